```python
import math
import jax, jax.numpy as jnp
from jax import lax
import numpy as np

D_MODEL = 1024
BATCH = 16
SEQ = 256
DEPTH = 1
DEC_BATCH = 2
DEC_SEQ = 2048
PAST_LEN = 256

GRID_W = 64
F_WIDTH = D_MODEL
F_GROUPS = 4
F_GROUP_W = F_WIDTH // F_GROUPS
D_INNER = 2 * D_MODEL
SSD_HEAD_DIM = 64
SSD_HEADS = D_INNER // SSD_HEAD_DIM
SSD_GROUPS = 8
D_STATE = 128
CONV_K = 3
CONV_DIM = D_INNER + 2 * SSD_GROUPS * D_STATE
CHUNK = 128
EPS = 1e-6
IN_WIDTH = 2 * F_WIDTH + D_INNER + CONV_DIM + 2 * SSD_HEADS + 2 * D_MODEL

kernel_name = "hybrid_fnet_ssd_prefix_diffusion_step"


def rmsnorm(x, w):
    xf = x.astype(jnp.float32)
    y = xf * lax.rsqrt(jnp.mean(xf * xf, axis=-1, keepdims=True) + EPS)
    return y.astype(x.dtype) * w


def grouped_rmsnorm(x, w, groups):
    shp = x.shape
    xf = x.astype(jnp.float32).reshape(shp[:-1] + (groups, shp[-1] // groups))
    y = xf * lax.rsqrt(jnp.mean(xf * xf, axis=-1, keepdims=True) + EPS)
    return y.reshape(shp).astype(x.dtype) * w


def centred_dwconv(u, w, b):
    out = lax.conv_general_dilated(
        u, w[:, None, :].astype(u.dtype), window_strides=(1,),
        padding=[(CONV_K // 2, CONV_K // 2)],
        dimension_numbers=("NWC", "WIO", "NWC"),
        feature_group_count=u.shape[-1])
    return out + b


def fourier_mix(u, on_grid):
    bt, L, _ = u.shape
    uf = u.astype(jnp.float32)
    if on_grid:
        rows = L // GRID_W
        uf = uf.reshape(bt, rows, GRID_W, F_GROUPS, F_GROUP_W)
        axes = (1, 2, 4)
    else:
        uf = uf.reshape(bt, L, F_GROUPS, F_GROUP_W)
        axes = (1, 3)
    out = jnp.fft.fftn(uf, axes=axes, norm="ortho").real
    return out.reshape(bt, L, F_WIDTH).astype(u.dtype)


def ssd_scan(x, dt, a, b_mat, c_mat, init_state):
    bt, L, H, P = x.shape
    G, N = b_mat.shape[-2], b_mat.shape[-1]
    R = H // G
    nc = L // CHUNK
    xf = x.astype(jnp.float32)
    dtf = dt.astype(jnp.float32)
    xdt = (xf * dtf[..., None]).reshape(bt, nc, CHUNK, G, R, P)
    bc = b_mat.astype(jnp.float32).reshape(bt, nc, CHUNK, G, N)
    cc = c_mat.astype(jnp.float32).reshape(bt, nc, CHUNK, G, N)
    da = (dtf * a.astype(jnp.float32)).reshape(bt, nc, CHUNK, G, R).transpose(0, 3, 4, 1, 2)
    acs = jnp.cumsum(da, axis=-1)
    seg = acs[..., :, None] - acs[..., None, :]
    tri = jnp.tril(jnp.ones((CHUNK, CHUNK), dtype=bool))
    lmat = jnp.exp(jnp.where(tri, seg, -jnp.inf))
    y_diag = jnp.einsum("bclgn,bcsgn,bgrcls,bcsgrp->bclgrp", cc, bc, lmat, xdt)
    decay_states = jnp.exp(acs[..., -1:] - acs)
    chunk_states = jnp.einsum("bcsgn,bgrcs,bcsgrp->bcgrpn", bc, decay_states, xdt)
    chunk_decay = jnp.exp(acs[..., -1])

    def step(s, inp):
        cs, cd = inp
        return s * cd[..., None, None] + cs, s

    s0 = init_state.astype(jnp.float32).reshape(bt, G, R, P, N)
    final, states_in = lax.scan(step, s0, (jnp.moveaxis(chunk_states, 1, 0), jnp.moveaxis(chunk_decay, 3, 0)))
    states_in = jnp.moveaxis(states_in, 0, 1)
    y_off = jnp.einsum("bclgn,bcgrpn,bgrcl->bclgrp", cc, states_in, jnp.exp(acs))
    y = (y_diag + y_off).reshape(bt, L, H, P).astype(x.dtype)
    return y, final.reshape(bt, H, P, N)


def bidir_ssd(xs, dt_f, dt_b, a_f, a_b, b_mat, c_mat, init_f, init_b):
    flip = lambda t: jnp.flip(t, axis=1)
    y_f, s_f = ssd_scan(xs, dt_f, a_f, b_mat, c_mat, init_f)
    y_b, s_b = ssd_scan(flip(xs), flip(dt_b), a_b, flip(b_mat), flip(c_mat), init_b)
    return y_f + flip(y_b), s_f, s_b


def trunk_layer(x, cond, init_f, init_b, on_grid, ada_w, ada_b, pre_w, post_w, w_in, conv_w, conv_b,
                dtb_f, dtb_b, alog_f, alog_b, d_skip, ssd_norm_w, fnet_w, w_branch_f, w_branch_s, w_out):
    bt, L, _ = x.shape
    ada = jax.nn.silu(cond) @ ada_w + ada_b
    shift, scale, gate = jnp.split(ada[:, None, :], 3, axis=-1)
    h = rmsnorm(x, pre_w) * (1.0 + scale) + shift
    p = h @ w_in
    cuts = np.cumsum([F_WIDTH, F_WIDTH, D_INNER, CONV_DIM, SSD_HEADS, SSD_HEADS, D_MODEL])
    u_f, g_f, z, xbc, dt_f, dt_b, mg_f, mg_s = jnp.split(p, cuts, axis=-1)

    f_out = ((fourier_mix(u_f, on_grid) @ fnet_w) * jax.nn.silu(g_f)) @ w_branch_f

    xbc = jax.nn.silu(centred_dwconv(xbc, conv_w, conv_b))
    xs, b_mat, c_mat = jnp.split(xbc, [D_INNER, D_INNER + SSD_GROUPS * D_STATE], axis=-1)
    xs = xs.reshape(bt, L, SSD_HEADS, SSD_HEAD_DIM)
    b_mat = b_mat.reshape(bt, L, SSD_GROUPS, D_STATE)
    c_mat = c_mat.reshape(bt, L, SSD_GROUPS, D_STATE)
    dtf = jax.nn.softplus(dt_f.astype(jnp.float32) + dtb_f)
    dtb = jax.nn.softplus(dt_b.astype(jnp.float32) + dtb_b)
    a_f = -jnp.exp(alog_f.astype(jnp.float32))
    a_b = -jnp.exp(alog_b.astype(jnp.float32))
    y, s_f, s_b = bidir_ssd(xs, dtf, dtb, a_f, a_b, b_mat, c_mat, init_f, init_b)
    y = (y + xs * d_skip[:, None]).reshape(bt, L, D_INNER)
    y = grouped_rmsnorm(y * jax.nn.silu(z), ssd_norm_w, SSD_GROUPS)
    s_out = y @ w_branch_s

    merged = jax.nn.sigmoid(mg_f) * f_out + jax.nn.sigmoid(mg_s) * s_out
    o = merged @ w_out
    return x + gate * rmsnorm(o, post_w), s_f, s_b


def setup_inputs(seed: int = 0) -> dict:
    key = jax.random.key(seed)
    ks = jax.random.split(key, 26)
    nrm = lambda k, shape, s: jax.random.normal(k, shape, jnp.float32) * s

    def dt_bias(k):
        u = jax.random.uniform(k, (DEPTH, SSD_HEADS), jnp.float32)
        dt0 = jnp.exp(u * (math.log(0.1) - math.log(0.001)) + math.log(0.001))
        return dt0 + jnp.log(-jnp.expm1(-dt0))

    st_shape = (DEC_BATCH, DEPTH, SSD_HEADS, SSD_HEAD_DIM, D_STATE)
    return {
        "x_prompt": nrm(ks[0], (BATCH, SEQ, D_MODEL), 1.0),
        "x_sample": nrm(ks[1], (DEC_BATCH, DEC_SEQ, D_MODEL), 1.0),
        "c": nrm(ks[2], (DEC_BATCH, D_MODEL), 1.0),
        "state_ssd_fwd": nrm(ks[3], st_shape, 0.05),
        "state_ssd_bwd": nrm(ks[4], st_shape, 0.05),
        "c_ctx": nrm(ks[5], (D_MODEL,), 1.0),
        "ada_w": nrm(ks[6], (DEPTH, D_MODEL, 3 * D_MODEL), 0.5 * D_MODEL ** -0.5),
        "ada_b": nrm(ks[7], (DEPTH, 3 * D_MODEL), 0.02),
        "pre_norm_w": 1.0 + nrm(ks[8], (DEPTH, D_MODEL), 0.1),
        "post_norm_w": 1.0 + nrm(ks[9], (DEPTH, D_MODEL), 0.1),
        "w_in": nrm(ks[10], (DEPTH, D_MODEL, IN_WIDTH), D_MODEL ** -0.5),
        "conv_w": nrm(ks[11], (DEPTH, CONV_K, CONV_DIM), CONV_K ** -0.5),
        "conv_b": nrm(ks[12], (DEPTH, CONV_DIM), 0.02),
        "dt_bias_fwd": dt_bias(ks[13]),
        "dt_bias_bwd": dt_bias(ks[14]),
        "a_log_fwd": jnp.log(jax.random.uniform(ks[15], (DEPTH, SSD_HEADS), jnp.float32, 1.0, 16.0)),
        "a_log_bwd": jnp.log(jax.random.uniform(ks[16], (DEPTH, SSD_HEADS), jnp.float32, 1.0, 16.0)),
        "d_skip": 1.0 + nrm(ks[17], (DEPTH, SSD_HEADS), 0.1),
        "ssd_norm_w": 1.0 + nrm(ks[18], (DEPTH, D_INNER), 0.1),
        "fnet_w": nrm(ks[19], (DEPTH, F_WIDTH, F_WIDTH), F_WIDTH ** -0.5),
        "w_branch_f": nrm(ks[20], (DEPTH, F_WIDTH, D_MODEL), F_WIDTH ** -0.5),
        "w_branch_s": nrm(ks[21], (DEPTH, D_INNER, D_MODEL), D_INNER ** -0.5),
        "w_out": nrm(ks[22], (DEPTH, D_MODEL, D_MODEL), D_MODEL ** -0.5),
    }


def reference(x_prompt, x_sample, c, state_ssd_fwd, state_ssd_bwd, c_ctx, ada_w, ada_b, pre_norm_w,
              post_norm_w, w_in, conv_w, conv_b, dt_bias_fwd, dt_bias_bwd, a_log_fwd, a_log_bwd, d_skip,
              ssd_norm_w, fnet_w, w_branch_f, w_branch_s, w_out):
    ctx_cond = c_ctx[None, :]
    zero_state = jnp.zeros((x_prompt.shape[0], SSD_HEADS, SSD_HEAD_DIM, D_STATE), jnp.float32)
    xp = x_prompt
    xl = x_sample
    new_f = []
    new_b = []
    for i in range(DEPTH):
        lw = (ada_w[i], ada_b[i], pre_norm_w[i], post_norm_w[i], w_in[i], conv_w[i], conv_b[i],
              dt_bias_fwd[i], dt_bias_bwd[i], a_log_fwd[i], a_log_bwd[i], d_skip[i], ssd_norm_w[i],
              fnet_w[i], w_branch_f[i], w_branch_s[i], w_out[i])
        xp, sf, sb = trunk_layer(xp, ctx_cond, zero_state, zero_state, False, *lw)
        new_f.append(sf)
        new_b.append(sb)
        xl, _, _ = trunk_layer(xl, c, state_ssd_fwd[:, i], state_ssd_bwd[:, i], True, *lw)
    new_state_ssd_fwd = jnp.stack(new_f, axis=1)
    new_state_ssd_bwd = jnp.stack(new_b, axis=1)
    return (xp, xl, new_state_ssd_fwd, new_state_ssd_bwd)
```

```python
import functools
import math

import numpy as np
import jax
import jax.numpy as jnp
from jax import lax
from jax.experimental import pallas as pl
from jax.experimental.pallas import tpu as pltpu

F32 = jnp.float32
BF16 = jnp.bfloat16

D_MODEL = 1024
GRID_W = 64
F_GROUPS = 4
F_GROUP_W = D_MODEL // F_GROUPS
D_INNER = 2 * D_MODEL
HEAD_DIM = 64
N_HEADS = D_INNER // HEAD_DIM
N_GROUPS = 8
HEADS_PER_GROUP = N_HEADS // N_GROUPS
GROUP_W = HEADS_PER_GROUP * HEAD_DIM
D_STATE = 128
CONV_DIM = D_INNER + 2 * N_GROUPS * D_STATE
CHUNK = 128
EPS = 1e-6
LOG2E = 1.4426950408889634

LANES = 128
DT_W = LANES
P_MAIN_W = 2 * D_MODEL + D_INNER + CONV_DIM + 2 * D_MODEL
COL_Z = 2 * D_MODEL
COL_XBC = COL_Z + D_INNER
COL_MG = COL_XBC + CONV_DIM
XC_B = D_INNER
XC_C = D_INNER + N_GROUPS * D_STATE

SSD_STEP = 2 * CHUNK
HALO_ROWS = 16
VMEM_LIMIT = 56 * 1024 * 1024


def _dot(a, b):
    return jnp.dot(a, b, preferred_element_type=F32)


def _sigmoid(x):
    return 1.0 / (1.0 + jnp.exp(-x))


def _silu(x):
    return x * _sigmoid(x)


def _softplus(x):
    return jnp.maximum(x, 0.0) + jnp.log(1.0 + jnp.exp(-jnp.abs(x)))


def _split2(v):
    hi = v.astype(BF16)
    lo = (v - hi.astype(F32)).astype(BF16)
    return hi, lo


def _split3(v):
    hi = v.astype(BF16)
    r1 = v - hi.astype(F32)
    mid = r1.astype(BF16)
    lo = (r1 - mid.astype(F32)).astype(BF16)
    return hi, mid, lo


def _ada_kernel(cond_ref, w_ref, b_ref, o_ref):
    c = cond_ref[...]
    s_hi, s_lo = _split2(_silu(c))
    w_hi, w_lo = _split2(w_ref[...])
    acc = _dot(s_hi, w_hi) + _dot(s_lo, w_hi) + _dot(s_hi, w_lo)
    o_ref[...] = acc + b_ref[...]


def _ada(cond8, ada_w, ada_b):
    tn = 512
    n = ada_w.shape[1]
    return pl.pallas_call(
        _ada_kernel,
        grid=(n // tn,),
        in_specs=[
            pl.BlockSpec((8, D_MODEL), lambda j: (0, 0)),
            pl.BlockSpec((D_MODEL, tn), lambda j: (0, j)),
            pl.BlockSpec((1, tn), lambda j: (0, j)),
        ],
        out_specs=pl.BlockSpec((8, tn), lambda j: (0, j)),
        out_shape=jax.ShapeDtypeStruct((8, n), F32),
        name="ada",
    )(cond8, ada_w, ada_b)


INPROJ_TM = 1024
INPROJ_TN = 1024


def _inproj_kernel(x_ref, mod_ref, prew_ref, w_ref, wdt_ref, dtb_ref, p_ref, dt_ref, h_ref):
    j = pl.program_id(1)

    @pl.when(j == 0)
    def _():
        x = x_ref[...]
        ms = jnp.mean(x * x, axis=-1, keepdims=True)
        xn = x * lax.rsqrt(ms + EPS) * prew_ref[...]
        mod = mod_ref[...]
        shift = mod[:, :D_MODEL]
        scale = mod[:, D_MODEL:2 * D_MODEL]
        hb = (xn * (1.0 + scale) + shift).astype(BF16)
        h_ref[...] = hb
        dt_ref[...] = _softplus(_dot(hb, wdt_ref[...]) + dtb_ref[...])

    acc = _dot(h_ref[...], w_ref[...])
    col = j * INPROJ_TN
    is_silu = (col >= D_MODEL) & (col < COL_XBC)
    is_sig = col >= COL_MG

    @pl.when(is_silu)
    def _():
        p_ref[...] = _silu(acc).astype(BF16)

    @pl.when(is_sig)
    def _():
        p_ref[...] = _sigmoid(acc).astype(BF16)

    @pl.when(jnp.logical_not(is_silu | is_sig))
    def _():
        p_ref[...] = acc.astype(BF16)


def _inproj(x2d, mod, rows_per_mod, pre_w, w_main, w_dt, dt_bias):
    m = x2d.shape[0]
    tm, tn = INPROJ_TM, INPROJ_TN
    tiles_per_mod = rows_per_mod // tm
    return pl.pallas_call(
        _inproj_kernel,
        grid=(m // tm, P_MAIN_W // tn),
        in_specs=[
            pl.BlockSpec((tm, D_MODEL), lambda i, j: (i, 0)),
            pl.BlockSpec((None, 1, 3 * D_MODEL), lambda i, j: (i // tiles_per_mod, 0, 0)),
            pl.BlockSpec((1, D_MODEL), lambda i, j: (0, 0)),
            pl.BlockSpec((D_MODEL, tn), lambda i, j: (0, j)),
            pl.BlockSpec((D_MODEL, DT_W), lambda i, j: (0, 0)),
            pl.BlockSpec((1, DT_W), lambda i, j: (0, 0)),
        ],
        out_specs=[
            pl.BlockSpec((tm, tn), lambda i, j: (i, j)),
            pl.BlockSpec((tm, DT_W), lambda i, j: (i, 0)),
        ],
        out_shape=[
            jax.ShapeDtypeStruct((m, P_MAIN_W), BF16),
            jax.ShapeDtypeStruct((m, DT_W), F32),
        ],
        scratch_shapes=[pltpu.VMEM((tm, D_MODEL), BF16)],
        compiler_params=pltpu.CompilerParams(
            dimension_semantics=("arbitrary", "arbitrary"),
            vmem_limit_bytes=VMEM_LIMIT),
        name="inproj",
    )(x2d, mod, pre_w, w_main, w_dt, dt_bias)


def _dft_cos_sin(n):
    k = np.arange(n)
    ang = 2.0 * np.pi * np.outer(k, k) / n
    return np.cos(ang), np.sin(ang)


def _channel_dft():
    c, s = _dft_cos_sin(F_GROUP_W)
    scale = 1.0 / math.sqrt(F_GROUP_W)
    return jnp.asarray(np.concatenate([c, s], axis=1) * scale, dtype=F32)


def _prompt_pos_dft(seq):
    c, s = _dft_cos_sin(seq)
    scale = 1.0 / math.sqrt(seq)
    return jnp.asarray(np.concatenate([c, -s], axis=1) * scale, dtype=F32)


def _grid_pos_dft(seq):
    rows = seq // GRID_W
    cr, sr = _dft_cos_sin(rows)
    cc, sc = _dft_cos_sin(GRID_W)
    scale = 1.0 / math.sqrt(seq)
    cr, sr, cc, sc = (jnp.asarray(t, dtype=F32) for t in (cr, sr, cc, sc))

    def kron(a, b):
        return (a[:, None, :, None] * b[None, :, None, :]).reshape(seq, seq)

    pc = kron(cr, cc) - kron(sr, sc)
    ps = kron(sr, cc) + kron(cr, sc)
    return jnp.concatenate([pc, -ps], axis=1) * scale


def _fourier_prompt_kernel(u_ref, g_ref, cch_ref, pos_ref, fw_ref, bw_ref, o_ref, st_ref, y_ref, *, seq, nseq):
    gw = F_GROUP_W
    for gch in range(F_GROUPS):
        ab = _dot(u_ref[:, gch * gw:(gch + 1) * gw], cch_ref[...].astype(BF16)).astype(BF16)
        for s in range(nseq):
            st_ref[s, 0:seq, gch * gw:(gch + 1) * gw] = ab[s * seq:(s + 1) * seq, 0:gw]
            st_ref[s, seq:2 * seq, gch * gw:(gch + 1) * gw] = ab[s * seq:(s + 1) * seq, gw:2 * gw]
    for s in range(nseq):
        y_ref[s * seq:(s + 1) * seq, :] = _dot(pos_ref[...].astype(BF16),st_ref[s]).astype(BF16)
    z = _dot(y_ref[...], fw_ref[...]) * g_ref[...].astype(F32)
    o_ref[...] = _dot(z.astype(BF16), bw_ref[...]).astype(BF16)


def _fourier_prompt(p_main, seq, cch, pos, fnet_w, w_branch_f):
    m = p_main.shape[0]
    nseq = 4
    rows = nseq * seq
    kern = functools.partial(_fourier_prompt_kernel, seq=seq, nseq=nseq)
    return pl.pallas_call(
        kern,
        grid=(m // rows,),
        in_specs=[
            pl.BlockSpec((rows, D_MODEL), lambda i: (i, 0)),
            pl.BlockSpec((rows, D_MODEL), lambda i: (i, 1)),
            pl.BlockSpec((F_GROUP_W, 2 * F_GROUP_W), lambda i: (0, 0)),
            pl.BlockSpec((seq, 2 * seq), lambda i: (0, 0)),
            pl.BlockSpec((D_MODEL, D_MODEL), lambda i: (0, 0)),
            pl.BlockSpec((D_MODEL, D_MODEL), lambda i: (0, 0)),
        ],
        out_specs=pl.BlockSpec((rows, D_MODEL), lambda i: (i, 0)),
        out_shape=jax.ShapeDtypeStruct((m, D_MODEL), BF16),
        scratch_shapes=[
            pltpu.VMEM((nseq, 2 * seq, D_MODEL), BF16),
            pltpu.VMEM((rows, D_MODEL), BF16),
        ],
        compiler_params=pltpu.CompilerParams(
            dimension_semantics=("arbitrary",), vmem_limit_bytes=VMEM_LIMIT),
        name="fourier_prompt",
    )(p_main, p_main, cch, pos, fnet_w, w_branch_f)


FOURIER_ROW_TILE = 512


def _fourier_grid_kernel(u_ref, g_ref, cch_ref, pos_ref, fw_ref, bw_ref, o_ref, st_ref, *, seq):
    gw = F_GROUP_W

    @pl.when(pl.program_id(1) == 0)
    def _():
        for gch in range(F_GROUPS):
            ab = _dot(u_ref[:, gch * gw:(gch + 1) * gw], cch_ref[...].astype(BF16)).astype(BF16)
            st_ref[0:seq, gch * gw:(gch + 1) * gw] = ab[:, 0:gw]
            st_ref[seq:2 * seq, gch * gw:(gch + 1) * gw] = ab[:, gw:2 * gw]

    y = _dot(pos_ref[...].astype(BF16),st_ref[...]).astype(BF16)
    z = _dot(y, fw_ref[...]) * g_ref[...].astype(F32)
    o_ref[...] = _dot(z.astype(BF16), bw_ref[...]).astype(BF16)


def _fourier_grid(p_main, seq, cch, pos, fnet_w, w_branch_f):
    m = p_main.shape[0]
    rt = FOURIER_ROW_TILE
    tiles = seq // rt
    kern = functools.partial(_fourier_grid_kernel, seq=seq)
    return pl.pallas_call(
        kern,
        grid=(m // seq, tiles),
        in_specs=[
            pl.BlockSpec((seq, D_MODEL), lambda b, r: (b, 0)),
            pl.BlockSpec((rt, D_MODEL), lambda b, r: (b * tiles + r, 1)),
            pl.BlockSpec((F_GROUP_W, 2 * F_GROUP_W), lambda b, r: (0, 0)),
            pl.BlockSpec((rt, 2 * seq), lambda b, r: (r, 0)),
            pl.BlockSpec((D_MODEL, D_MODEL), lambda b, r: (0, 0)),
            pl.BlockSpec((D_MODEL, D_MODEL), lambda b, r: (0, 0)),
        ],
        out_specs=pl.BlockSpec((rt, D_MODEL), lambda b, r: (b * tiles + r, 0)),
        out_shape=jax.ShapeDtypeStruct((m, D_MODEL), BF16),
        scratch_shapes=[pltpu.VMEM((2 * seq, D_MODEL), BF16)],
        compiler_params=pltpu.CompilerParams(
            dimension_semantics=("arbitrary", "arbitrary"), vmem_limit_bytes=VMEM_LIMIT),
        name="fourier_grid",
    )(p_main, p_main, cch, pos, fnet_w, w_branch_f)


def _head_expand_matrix(lane_offset):
    e = np.zeros((2 * LANES, D_INNER), np.float32)
    for h in range(N_HEADS):
        e[lane_offset + h, h * HEAD_DIM:(h + 1) * HEAD_DIM] = 1.0
        e[LANES + lane_offset + h, h * HEAD_DIM:(h + 1) * HEAD_DIM] = 1.0
    return jnp.asarray(e, dtype=BF16)


def _expand_heads(v, e):
    hi, lo = _split2(v)
    return _dot(jnp.concatenate([hi, lo], axis=1), e)


def _cumsum_rows(tri, v):
    hi, mid, lo = _split3(v)
    s = _dot(tri, jnp.concatenate([hi, mid, lo], axis=1))
    return s[:, :LANES] + s[:, LANES:2 * LANES] + s[:, 2 * LANES:]


def _transpose_bf16(x):
    return x.astype(F32).T.astype(BF16)


def _load_state_t(st_ref, src_ref):
    for g in range(N_GROUPS):
        st_ref[g] = src_ref[g * GROUP_W:(g + 1) * GROUP_W, :].T


def _store_state(dst_ref, st_ref):
    for g in range(N_GROUPS):
        dst_ref[g * GROUP_W:(g + 1) * GROUP_W, :] = st_ref[g].T


def _sweep1_kernel(*refs, nst, has_init, out_state):
    it = iter(refs)
    xbc_ref, xprev_ref, xnext_ref, dt_ref, cw_ref, cb_ref, alog_ref, e_ref = (next(it) for _ in range(8))
    init_ref = next(it) if has_init else None
    xc_ref, yb_ref = next(it), next(it)
    sb_ref = next(it) if out_state else None
    st_ref = next(it)

    j = pl.program_id(1)
    k = nst - 1 - j
    t = SSD_STEP
    q = CHUNK

    @pl.when(j == 0)
    def _():
        if has_init:
            _load_state_t(st_ref, init_ref)
        else:
            st_ref[...] = jnp.zeros_like(st_ref)

    rows = lax.broadcasted_iota(jnp.int32, (t, 1), 0)
    prev_on = (k > 0).astype(F32)
    next_on = (k < nst - 1).astype(F32)
    slab = 512
    for c0 in range(0, CONV_DIM, slab):
        x = xbc_ref[:, c0:c0 + slab].astype(F32)
        prev_row = xprev_ref[:, c0:c0 + slab].astype(F32)[HALO_ROWS - 1:HALO_ROWS, :] * prev_on
        next_row = xnext_ref[:, c0:c0 + slab].astype(F32)[0:1, :] * next_on
        xm1 = jnp.where(rows == 0, prev_row, pltpu.roll(x, 1, axis=0))
        xp1 = jnp.where(rows == t - 1, next_row, pltpu.roll(x, t - 1, axis=0))
        w = cw_ref[:, c0:c0 + slab]
        y = xm1 * w[0:1, :] + x * w[1:2, :] + xp1 * w[2:3, :] + cb_ref[:, c0:c0 + slab]
        xc_ref[:, c0:c0 + slab] = _silu(y).astype(BF16)

    a = -jnp.exp(alog_ref[...])
    ri = lax.broadcasted_iota(jnp.int32, (q, q), 0)
    ci = lax.broadcasted_iota(jnp.int32, (q, q), 1)
    tri_u = (ri <= ci).astype(BF16)

    for cc in reversed(range(t // q)):
        r0 = cc * q
        dt = dt_ref[r0:r0 + q, :]
        racs = _cumsum_rows(tri_u, dt * a)
        eb_x = _expand_heads(jnp.exp(racs), e_ref[...])
        wb_x = _expand_heads(dt * jnp.exp(racs[0:1, :] - racs), e_ref[...])
        for g in range(N_GROUPS):
            gs = slice(g * GROUP_W, (g + 1) * GROUP_W)
            cg = xc_ref[r0:r0 + q, XC_C + g * D_STATE:XC_C + (g + 1) * D_STATE]
            bg = xc_ref[r0:r0 + q, XC_B + g * D_STATE:XC_B + (g + 1) * D_STATE]
            stg = st_ref[g]
            yo = _dot(cg, stg.astype(BF16)) * eb_x[:, gs]
            yb_ref[r0:r0 + q, gs] = yo.astype(BF16)
            xw = (xc_ref[r0:r0 + q, gs].astype(F32) * wb_x[:, gs]).astype(BF16)
            st_ref[g] = stg * eb_x[0:1, gs] + _dot(_transpose_bf16(bg), xw)

    if out_state:
        @pl.when(j == nst - 1)
        def _():
            _store_state(sb_ref, st_ref)


def _sweep1(p_main, dt, seq, conv_w, conv_b, alog, e_b, init_state, out_state):
    m = p_main.shape[0]
    t = SSD_STEP
    nst = seq // t
    nseq = m // seq
    hb = t // HALO_ROWS
    last_halo = m // HALO_ROWS - 1
    xbc_col = COL_XBC // CONV_DIM
    has_init = init_state is not None

    def blk(s, j):
        return s * nst + (nst - 1 - j)

    in_specs = [
        pl.BlockSpec((t, CONV_DIM), lambda s, j: (blk(s, j), xbc_col)),
        pl.BlockSpec((HALO_ROWS, CONV_DIM), lambda s, j: (jnp.maximum(blk(s, j) * hb - 1, 0), xbc_col)),
        pl.BlockSpec((HALO_ROWS, CONV_DIM), lambda s, j: (jnp.minimum((blk(s, j) + 1) * hb, last_halo), xbc_col)),
        pl.BlockSpec((t, DT_W), lambda s, j: (blk(s, j), 0)),
        pl.BlockSpec((3, CONV_DIM), lambda s, j: (0, 0)),
        pl.BlockSpec((1, CONV_DIM), lambda s, j: (0, 0)),
        pl.BlockSpec((1, DT_W), lambda s, j: (0, 0)),
        pl.BlockSpec((2 * LANES, D_INNER), lambda s, j: (0, 0)),
    ]
    args = [p_main, p_main, p_main, dt, conv_w, conv_b, alog, e_b]
    if has_init:
        in_specs.append(pl.BlockSpec((None, D_INNER, D_STATE), lambda s, j: (s, 0, 0)))
        args.append(init_state)
    out_specs = [
        pl.BlockSpec((t, CONV_DIM), lambda s, j: (blk(s, j), 0)),
        pl.BlockSpec((t, D_INNER), lambda s, j: (blk(s, j), 0)),
    ]
    out_shape = [
        jax.ShapeDtypeStruct((m, CONV_DIM), BF16),
        jax.ShapeDtypeStruct((m, D_INNER), BF16),
    ]
    if out_state:
        out_specs.append(pl.BlockSpec((None, D_INNER, D_STATE), lambda s, j: (s, 0, 0)))
        out_shape.append(jax.ShapeDtypeStruct((nseq, D_INNER, D_STATE), F32))
    kern = functools.partial(_sweep1_kernel, nst=nst, has_init=has_init, out_state=out_state)
    return pl.pallas_call(
        kern,
        grid=(nseq, nst),
        in_specs=in_specs,
        out_specs=out_specs,
        out_shape=out_shape,
        scratch_shapes=[pltpu.VMEM((N_GROUPS, D_STATE, GROUP_W), F32)],
        compiler_params=pltpu.CompilerParams(
            dimension_semantics=("arbitrary", "arbitrary"), vmem_limit_bytes=VMEM_LIMIT),
        name="ssd_sweep1",
    )(*args)


def _sweep2_kernel(*refs, nst, has_init, out_state):
    it = iter(refs)
    (xc_ref, sz_ref, dt_ref, yb_ref, fo_ref, mg_ref, x_ref, mod_ref, alog_ref, dskip_ref, nw_ref,
     e_ref, wbs_ref, wout_ref, postw_ref) = (next(it) for _ in range(15))
    init_ref = next(it) if has_init else None
    out_ref = next(it)
    sf_ref = next(it) if out_state else None
    st_ref, yn_ref = next(it), next(it)

    j = pl.program_id(1)
    t = SSD_STEP
    q = CHUNK

    @pl.when(j == 0)
    def _():
        if has_init:
            _load_state_t(st_ref, init_ref)
        else:
            st_ref[...] = jnp.zeros_like(st_ref)

    a = -jnp.exp(alog_ref[...])
    lane = lax.broadcasted_iota(jnp.int32, (q, LANES), 1)
    ri = lax.broadcasted_iota(jnp.int32, (q, q), 0)
    ci = lax.broadcasted_iota(jnp.int32, (q, q), 1)
    lower = ri > ci
    upper = ri < ci
    tri_l = (ri >= ci).astype(BF16)
    tri_u = (ri <= ci).astype(BF16)
    first_head = lane < HEAD_DIM

    for cc in range(t // q):
        r0 = cc * q
        dt = dt_ref[r0:r0 + q, :]
        da = dt * a
        acs = jnp.where(lane < N_HEADS, _cumsum_rows(tri_l, da), _cumsum_rows(tri_u, da))
        cf = acs * LOG2E
        dtsum = dt + pltpu.roll(dt, LANES - N_HEADS, axis=1)
        diag = pltpu.roll(jnp.log2(dtsum), 2 * N_HEADS, axis=1)
        rt_ = jnp.where(lane < 2 * N_HEADS, cf - jnp.log2(dt), diag).T
        cff = jnp.where(lane < N_HEADS, cf, 0.0)
        ef_x = _expand_heads(jnp.exp2(cff), e_ref[...])
        wf_x = _expand_heads(dt * jnp.exp2(cff[q - 1:q, :] - cff), e_ref[...])

        for g in range(N_GROUPS):
            gs = slice(g * GROUP_W, (g + 1) * GROUP_W)
            cg = xc_ref[r0:r0 + q, XC_C + g * D_STATE:XC_C + (g + 1) * D_STATE]
            bg_t = _transpose_bf16(xc_ref[r0:r0 + q, XC_B + g * D_STATE:XC_B + (g + 1) * D_STATE])
            gmat = _dot(cg, bg_t)
            pair_out = []
            for pr in range(HEADS_PER_GROUP // 2):
                h0 = g * HEADS_PER_GROUP + 2 * pr
                ms = []
                for h in (h0, h0 + 1):
                    e = jnp.where(
                        lower, cf[:, h:h + 1] - rt_[h:h + 1, :],
                        jnp.where(upper, cf[:, N_HEADS + h:N_HEADS + h + 1] - rt_[N_HEADS + h:N_HEADS + h + 1, :],
                                  rt_[2 * N_HEADS + h:2 * N_HEADS + h + 1, :]))
                    ms.append((gmat * jnp.exp2(e)).astype(BF16))
                xp = xc_ref[r0:r0 + q, h0 * HEAD_DIM:(h0 + 2) * HEAD_DIM]
                zero = jnp.zeros_like(xp)
                xbd = jnp.concatenate([jnp.where(first_head, xp, zero), jnp.where(first_head, zero, xp)], axis=0)
                pair_out.append(_dot(jnp.concatenate(ms, axis=1), xbd))
            stg = st_ref[g]
            xg = xc_ref[r0:r0 + q, gs].astype(F32)
            yg = (jnp.concatenate(pair_out, axis=1)
                  + _dot(cg, stg.astype(BF16)) * ef_x[:, gs]
                  + yb_ref[r0:r0 + q, gs].astype(F32)
                  + xg * dskip_ref[:, gs])
            st_ref[g] = stg * ef_x[q - 1:q, gs] + _dot(bg_t, (xg * wf_x[:, gs]).astype(BF16))
            yz = yg * sz_ref[r0:r0 + q, gs].astype(F32)
            ms_ = jnp.mean(yz * yz, axis=-1, keepdims=True)
            yn_ref[r0:r0 + q, gs] = (yz * lax.rsqrt(ms_ + EPS) * nw_ref[:, gs]).astype(BF16)

    s_out = _dot(yn_ref[...], wbs_ref[...])
    merged = (mg_ref[:, :D_MODEL].astype(F32) * fo_ref[...].astype(F32)
              + mg_ref[:, D_MODEL:].astype(F32) * s_out)
    o = _dot(merged.astype(BF16), wout_ref[...])
    on = o * lax.rsqrt(jnp.mean(o * o, axis=-1, keepdims=True) + EPS) * postw_ref[...]
    gate = mod_ref[...][:, 2 * D_MODEL:]
    out_ref[...] = x_ref[...] + gate * on

    if out_state:
        @pl.when(j == nst - 1)
        def _():
            _store_state(sf_ref, st_ref)


def _sweep2(xc, p_main, dt, yb, f_out, x2d, mod, seq, alog, dskip_x, norm_w, e_f, wbs, wout, post_w,
            init_state, out_state):
    m = xc.shape[0]
    t = SSD_STEP
    nst = seq // t
    nseq = m // seq
    has_init = init_state is not None

    def blk(s, j):
        return s * nst + j

    const = lambda s, j: (0, 0)
    in_specs = [
        pl.BlockSpec((t, CONV_DIM), lambda s, j: (blk(s, j), 0)),
        pl.BlockSpec((t, D_INNER), lambda s, j: (blk(s, j), COL_Z // D_INNER)),
        pl.BlockSpec((t, DT_W), lambda s, j: (blk(s, j), 0)),
        pl.BlockSpec((t, D_INNER), lambda s, j: (blk(s, j), 0)),
        pl.BlockSpec((t, D_MODEL), lambda s, j: (blk(s, j), 0)),
        pl.BlockSpec((t, 2 * D_MODEL), lambda s, j: (blk(s, j), COL_MG // (2 * D_MODEL))),
        pl.BlockSpec((t, D_MODEL), lambda s, j: (blk(s, j), 0)),
        pl.BlockSpec((None, 1, 3 * D_MODEL), lambda s, j: (s % mod.shape[0], 0, 0)),
        pl.BlockSpec((1, DT_W), const),
        pl.BlockSpec((1, D_INNER), const),
        pl.BlockSpec((1, D_INNER), const),
        pl.BlockSpec((2 * LANES, D_INNER), const),
        pl.BlockSpec((D_INNER, D_MODEL), const),
        pl.BlockSpec((D_MODEL, D_MODEL), const),
        pl.BlockSpec((1, D_MODEL), const),
    ]
    args = [xc, p_main, dt, yb, f_out, p_main, x2d, mod, alog, dskip_x, norm_w, e_f, wbs, wout, post_w]
    if has_init:
        in_specs.append(pl.BlockSpec((None, D_INNER, D_STATE), lambda s, j: (s, 0, 0)))
        args.append(init_state)
    out_specs = [pl.BlockSpec((t, D_MODEL), lambda s, j: (blk(s, j), 0))]
    out_shape = [jax.ShapeDtypeStruct((m, D_MODEL), F32)]
    if out_state:
        out_specs.append(pl.BlockSpec((None, D_INNER, D_STATE), lambda s, j: (s, 0, 0)))
        out_shape.append(jax.ShapeDtypeStruct((nseq, D_INNER, D_STATE), F32))
    kern = functools.partial(_sweep2_kernel, nst=nst, has_init=has_init, out_state=out_state)
    return pl.pallas_call(
        kern,
        grid=(nseq, nst),
        in_specs=in_specs,
        out_specs=out_specs,
        out_shape=out_shape,
        scratch_shapes=[
            pltpu.VMEM((N_GROUPS, D_STATE, GROUP_W), F32),
            pltpu.VMEM((t, D_INNER), BF16),
        ],
        compiler_params=pltpu.CompilerParams(
            dimension_semantics=("arbitrary", "arbitrary"), vmem_limit_bytes=VMEM_LIMIT),
        name="ssd_sweep2",
    )(*args)


def _token_set(x3d, mod, on_grid, init_f, init_b, want_state, wts):
    bt, seq, _ = x3d.shape
    x2d = x3d.reshape(bt * seq, D_MODEL)
    rows_per_mod = x2d.shape[0] if mod.shape[0] == 1 else seq
    p_main, dt = _inproj(x2d, mod, rows_per_mod, wts["pre_w"], wts["w_main"], wts["w_dt"], wts["dt_bias"])
    if on_grid:
        f_out = _fourier_grid(p_main, seq, wts["cch"], _grid_pos_dft(seq), wts["fnet_w"], wts["w_branch_f"])
    else:
        f_out = _fourier_prompt(p_main, seq, wts["cch"], _prompt_pos_dft(seq), wts["fnet_w"], wts["w_branch_f"])
    res1 = _sweep1(p_main, dt, seq, wts["conv_w"], wts["conv_b"], wts["alog"], wts["e_b"], init_b, want_state)
    xc, yb = res1[0], res1[1]
    res2 = _sweep2(xc, p_main, dt, yb, f_out, x2d, mod, seq, wts["alog"], wts["dskip_x"], wts["norm_w"],
                   wts["e_f"], wts["w_branch_s"], wts["w_out"], wts["post_w"], init_f, want_state)
    y = res2[0].reshape(bt, seq, D_MODEL)
    if want_state:
        shape = (bt, 1, N_HEADS, HEAD_DIM, D_STATE)
        return y, res2[1].reshape(shape), res1[2].reshape(shape)
    return y, None, None


def kernel(x_prompt, x_sample, c, state_ssd_fwd, state_ssd_bwd, c_ctx, ada_w, ada_b, pre_norm_w, post_norm_w, w_in, conv_w, conv_b, dt_bias_fwd, dt_bias_bwd, a_log_fwd, a_log_bwd, d_skip, ssd_norm_w, fnet_w, w_branch_f, w_branch_s, w_out):
    assert ada_w.shape[0] == 1, "single trunk layer"
    dec_b = x_sample.shape[0]

    cond8 = jnp.zeros((8, D_MODEL), F32).at[0].set(c_ctx).at[1:1 + dec_b].set(c)
    ada = _ada(cond8, ada_w[0], ada_b[0][None, :])
    mod_p = ada[0:1][:, None, :]
    mod_l = ada[1:1 + dec_b][:, None, :]

    w = w_in[0]
    col_dt = COL_XBC + CONV_DIM
    pad = jnp.zeros((1, DT_W - 2 * N_HEADS), F32)
    wts = dict(
        pre_w=pre_norm_w[0][None, :],
        post_w=post_norm_w[0][None, :],
        w_main=jnp.concatenate([w[:, :col_dt], w[:, col_dt + 2 * N_HEADS:]], axis=1).astype(BF16),
        w_dt=jnp.pad(w[:, col_dt:col_dt + 2 * N_HEADS], ((0, 0), (0, DT_W - 2 * N_HEADS))).astype(BF16),
        dt_bias=jnp.concatenate([dt_bias_fwd[0][None, :], dt_bias_bwd[0][None, :], pad], axis=1),
        alog=jnp.concatenate([a_log_fwd[0][None, :], a_log_bwd[0][None, :], pad], axis=1),
        conv_w=conv_w[0],
        conv_b=conv_b[0][None, :],
        dskip_x=jnp.repeat(d_skip[0], HEAD_DIM)[None, :],
        norm_w=ssd_norm_w[0][None, :],
        fnet_w=fnet_w[0].astype(BF16),
        w_branch_f=w_branch_f[0].astype(BF16),
        w_branch_s=w_branch_s[0].astype(BF16),
        w_out=w_out[0].astype(BF16),
        cch=_channel_dft(),
        e_f=_head_expand_matrix(0),
        e_b=_head_expand_matrix(N_HEADS),
    )

    y_p, s_f, s_b = _token_set(x_prompt, mod_p, False, None, None, True, wts)
    init_f = state_ssd_fwd[:, 0].reshape(dec_b, D_INNER, D_STATE)
    init_b = state_ssd_bwd[:, 0].reshape(dec_b, D_INNER, D_STATE)
    y_l, _, _ = _token_set(x_sample, mod_l, True, init_f, init_b, False, wts)
    return (y_p, y_l, s_f, s_b)
```
